```python
import jax, jax.numpy as jnp
from jax import lax
import numpy as np

D_MODEL = 2048
BATCH = 1
SEQ = 8192
DEPTH = 1

SGU_WIDTH = D_MODEL
SGU_GROUPS = 8
SGU_GROUP_DIM = SGU_WIDTH // SGU_GROUPS
SGU_CHUNK = 128
GLA_HEADS = 4
GLA_KEY_WIDTH = D_MODEL // 2
GLA_VALUE_WIDTH = D_MODEL
GLA_HEAD_K = GLA_KEY_WIDTH // GLA_HEADS
GLA_HEAD_V = GLA_VALUE_WIDTH // GLA_HEADS
GLA_GATE_RANK = 16
GLA_GATE_TEMP = 16.0
GLA_CHUNK = 64
PEER_HEADS = 8
PEER_N_KEYS = 128
PEER_N_EXPERTS = PEER_N_KEYS * PEER_N_KEYS
PEER_QUERY_DIM = 256
PEER_HALF = PEER_QUERY_DIM // 2
PEER_TOPK = 16
PEER_TOKEN_BLOCK = 64
EPS = 1e-6

IN_SPLIT_SIZES = (SGU_WIDTH, SGU_WIDTH,
                  GLA_KEY_WIDTH, GLA_KEY_WIDTH,
                  GLA_VALUE_WIDTH, GLA_VALUE_WIDTH,
                  GLA_GATE_RANK,
                  D_MODEL, D_MODEL)
IN_WIDTH = sum(IN_SPLIT_SIZES)

kernel_name = "hybrid_sgu_gla_peer_block"


def rmsnorm(x, g):
    xf = x.astype(jnp.float32)
    y = xf * lax.rsqrt(jnp.mean(xf * xf, axis=-1, keepdims=True) + EPS)
    return (y * g.astype(jnp.float32)).astype(x.dtype)


def layernorm(x, g, b):
    xf = x.astype(jnp.float32)
    mu = jnp.mean(xf, axis=-1, keepdims=True)
    var = jnp.mean(jnp.square(xf - mu), axis=-1, keepdims=True)
    y = (xf - mu) * lax.rsqrt(var + EPS)
    return (y * g.astype(jnp.float32) + b.astype(jnp.float32)).astype(x.dtype)


def split_columns(p):
    offsets = []
    acc = 0
    for s in IN_SPLIT_SIZES[:-1]:
        acc += s
        offsets.append(acc)
    return jnp.split(p, offsets, axis=-1)


def spatial_gating(u, v, w_s, b_s, ln_g, ln_b):
    B, S, _ = v.shape
    n = S // SGU_CHUNK
    v = layernorm(v, ln_g, ln_b).reshape(B, n, SGU_CHUNK, SGU_GROUPS, SGU_GROUP_DIM)
    causal = jnp.tril(jnp.ones((SGU_CHUNK, SGU_CHUNK), dtype=bool))
    w = jnp.where(causal[None], w_s, 0.0)
    s = jnp.einsum('gts,bnsgc->bntgc', w, v) + b_s.T[None, None, :, :, None]
    return u * s.reshape(B, S, SGU_WIDTH)


def gla_chunked(q, k, v, log_a):
    B, S, H, dk = q.shape
    dv = v.shape[-1]
    n = S // GLA_CHUNK

    def to_chunks(t):
        return t.astype(jnp.float32).reshape(B, n, GLA_CHUNK, H, t.shape[-1]).transpose(0, 3, 1, 2, 4)

    q = to_chunks(q) * (dk ** -0.5)
    k = to_chunks(k)
    v = to_chunks(v)
    b = jnp.cumsum(to_chunks(log_a), axis=-2)
    q_dec = q * jnp.exp(b)
    k_inv = k * jnp.exp(-b)
    causal = jnp.tril(jnp.ones((GLA_CHUNK, GLA_CHUNK), dtype=bool))
    att = jnp.where(causal, jnp.einsum('bhncd,bhnsd->bhncs', q_dec, k_inv), 0.0)
    o_intra = jnp.einsum('bhncs,bhnse->bhnce', att, v)
    b_last = b[..., -1, :]
    k_end = k * jnp.exp(b_last[..., None, :] - b)

    def step(state, inp):
        qc, kc, vc, dc = inp
        o = jnp.einsum('bhcd,bhde->bhce', qc, state)
        state = state * dc[..., None] + jnp.einsum('bhcd,bhce->bhde', kc, vc)
        return state, o

    xs = (jnp.moveaxis(q_dec, 2, 0), jnp.moveaxis(k_end, 2, 0),
          jnp.moveaxis(v, 2, 0), jnp.moveaxis(jnp.exp(b_last), 2, 0))
    _, o_inter = lax.scan(step, jnp.zeros((B, H, dk, dv), jnp.float32), xs)
    o = o_intra + jnp.moveaxis(o_inter, 0, 2)
    return o.transpose(0, 2, 3, 1, 4).reshape(B, S, H, dv)


def gla_branch(q, k, v, r, a_low, w_alpha_up, b_alpha, norm_g):
    B, S, _ = q.shape
    log_a = jax.nn.log_sigmoid((a_low @ w_alpha_up + b_alpha).astype(jnp.float32)) / GLA_GATE_TEMP
    o = gla_chunked(q.reshape(B, S, GLA_HEADS, GLA_HEAD_K),
                    k.reshape(B, S, GLA_HEADS, GLA_HEAD_K),
                    v.reshape(B, S, GLA_HEADS, GLA_HEAD_V),
                    log_a.reshape(B, S, GLA_HEADS, GLA_HEAD_K))
    o = rmsnorm(o, norm_g) * jax.nn.silu(r.astype(jnp.float32)).reshape(B, S, GLA_HEADS, GLA_HEAD_V)
    return o.reshape(B, S, GLA_VALUE_WIDTH).astype(q.dtype)


def peer(h, w_query, sub_keys, expert_u, expert_v):
    B, S, D = h.shape
    T = B * S
    hf = h.reshape(T, D)
    q = (hf @ w_query).reshape(T, PEER_HEADS, 2, PEER_HALF)
    s = jnp.einsum('thpd,hpkd->thpk', q, sub_keys).astype(jnp.float32)
    s1, i1 = lax.top_k(s[:, :, 0], PEER_TOPK)
    s2, i2 = lax.top_k(s[:, :, 1], PEER_TOPK)
    cand = (s1[..., :, None] + s2[..., None, :]).reshape(T, PEER_HEADS, PEER_TOPK * PEER_TOPK)
    cand_idx = (i1[..., :, None] * PEER_N_KEYS + i2[..., None, :]).reshape(T, PEER_HEADS, PEER_TOPK * PEER_TOPK)
    best, pos = lax.top_k(cand, PEER_TOPK)
    idx = jnp.take_along_axis(cand_idx, pos, axis=-1)
    gate = jax.nn.softmax(best, axis=-1).astype(h.dtype)

    nb = T // PEER_TOKEN_BLOCK

    def block(args):
        hb, ib, gb = args
        u = jnp.take(expert_u, ib, axis=0)
        act = jnp.einsum('td,thkd->thk', hb, u)
        wgt = gb * jax.nn.gelu(act, approximate=False)
        vv = jnp.take(expert_v, ib, axis=0)
        return jnp.einsum('thk,thkd->td', wgt, vv)

    out = lax.map(block, (hf.reshape(nb, PEER_TOKEN_BLOCK, D),
                          idx.reshape(nb, PEER_TOKEN_BLOCK, PEER_HEADS, PEER_TOPK),
                          gate.reshape(nb, PEER_TOKEN_BLOCK, PEER_HEADS, PEER_TOPK)))
    return out.reshape(B, S, D)


def setup_inputs(seed: int = 0) -> dict:
    key = jax.random.key(seed)
    ks = jax.random.split(key, 20)
    f32 = jnp.float32
    L, D = DEPTH, D_MODEL

    def nrm(k, shape, scale):
        return jax.random.normal(k, shape, f32) * scale

    return {
        "x": jax.random.normal(ks[0], (BATCH, SEQ, D), f32),
        "g_mix": 1.0 + nrm(ks[1], (L, D), 0.02),
        "w_in": nrm(ks[2], (L, D, IN_WIDTH), D ** -0.5),
        "w_s": nrm(ks[3], (L, SGU_GROUPS, SGU_CHUNK, SGU_CHUNK), SGU_CHUNK ** -0.5),
        "b_s": 1.0 + nrm(ks[4], (L, SGU_GROUPS, SGU_CHUNK), 0.02),
        "ln_v_g": 1.0 + nrm(ks[5], (L, SGU_WIDTH), 0.02),
        "ln_v_b": nrm(ks[6], (L, SGU_WIDTH), 0.02),
        "w_alpha_up": nrm(ks[7], (L, GLA_GATE_RANK, GLA_KEY_WIDTH), GLA_GATE_RANK ** -0.5),
        "b_alpha": nrm(ks[8], (L, GLA_KEY_WIDTH), 0.02),
        "gla_norm_g": 1.0 + nrm(ks[9], (L, GLA_HEAD_V), 0.02),
        "w_branch_a": nrm(ks[10], (L, SGU_WIDTH, D), SGU_WIDTH ** -0.5),
        "w_branch_b": nrm(ks[11], (L, GLA_VALUE_WIDTH, D), GLA_VALUE_WIDTH ** -0.5),
        "w_out": nrm(ks[12], (L, D, D), D ** -0.5),
        "g_ffn": 1.0 + nrm(ks[13], (L, D), 0.02),
        "w_query": nrm(ks[14], (L, D, PEER_HEADS * PEER_QUERY_DIM), D ** -0.5),
        "sub_keys": nrm(ks[15], (L, PEER_HEADS, 2, PEER_N_KEYS, PEER_HALF), PEER_HALF ** -0.5),
        "expert_u": nrm(ks[16], (L, PEER_N_EXPERTS, D), D ** -0.5),
        "expert_v": nrm(ks[17], (L, PEER_N_EXPERTS, D), D ** -0.5),
        "g_final": 1.0 + nrm(ks[18], (D,), 0.02),
    }


def reference(x, g_mix, w_in, w_s, b_s, ln_v_g, ln_v_b, w_alpha_up, b_alpha, gla_norm_g,
              w_branch_a, w_branch_b, w_out, g_ffn, w_query, sub_keys, expert_u, expert_v, g_final):
    for l in range(DEPTH):
        h = rmsnorm(x, g_mix[l])
        a_u, a_v, b_q, b_k, b_v, b_r, b_alow, gate_a, gate_b = split_columns(h @ w_in[l])
        y_a = spatial_gating(jax.nn.gelu(a_u, approximate=False), jax.nn.gelu(a_v, approximate=False),
                             w_s[l], b_s[l], ln_v_g[l], ln_v_b[l]) @ w_branch_a[l]
        y_b = gla_branch(b_q, b_k, b_v, b_r, b_alow, w_alpha_up[l], b_alpha[l], gla_norm_g[l]) @ w_branch_b[l]
        merged = jax.nn.sigmoid(gate_a) * y_a + jax.nn.sigmoid(gate_b) * y_b
        x = x + merged @ w_out[l]
        h2 = rmsnorm(x, g_ffn[l])
        x = x + peer(h2, w_query[l], sub_keys[l], expert_u[l], expert_v[l])
    return rmsnorm(x, g_final)
```

```python
import functools

import jax
import jax.numpy as jnp
from jax import lax
from jax.experimental import pallas as pl
from jax.experimental.pallas import tpu as pltpu

F32 = jnp.float32
BF16 = jnp.bfloat16

EPS = 1e-6
SGU_GROUPS = 8
SGU_CHUNK = 128
GLA_HEADS = 4
GLA_GATE_RANK = 16
GLA_GATE_TEMP = 16.0
GLA_CHUNK = 64
PEER_HEADS = 8
PEER_N_KEYS = 128
PEER_TOPK = 16
LANES = 128
VMEM_LIMIT = 56 * 1024 * 1024

NT_DIMS = (((1,), (1,)), ((), ()))
TN_DIMS = (((0,), (0,)), ((), ()))


def _params(semantics, vmem=VMEM_LIMIT):
    return pltpu.CompilerParams(dimension_semantics=semantics, vmem_limit_bytes=vmem)


def _gelu(x):
    return 0.5 * x * (1.0 + lax.erf(x * (2.0 ** -0.5)))


def _rms(x, g):
    return x * lax.rsqrt(jnp.mean(x * x, axis=-1, keepdims=True) + EPS) * g


def _in_proj_body(x_ref, g_ref, w_ref, wa_ref, p_ref, alow_ref, h_scr, *, j_gelu, j_none, j_silu):
    j = pl.program_id(1)

    @pl.when(j == 0)
    def _():
        hb = _rms(x_ref[...], g_ref[...]).astype(BF16)
        h_scr[...] = hb
        alow_ref[...] = jnp.dot(hb, wa_ref[...], preferred_element_type=F32)

    acc = jnp.dot(h_scr[...], w_ref[...], preferred_element_type=F32)

    @pl.when(j < j_gelu)
    def _():
        p_ref[...] = _gelu(acc).astype(BF16)

    @pl.when((j >= j_gelu) & (j < j_none))
    def _():
        p_ref[...] = acc.astype(BF16)

    @pl.when((j >= j_none) & (j < j_silu))
    def _():
        p_ref[...] = (acc * jax.nn.sigmoid(acc)).astype(BF16)

    @pl.when(j >= j_silu)
    def _():
        p_ref[...] = jax.nn.sigmoid(acc).astype(BF16)


def _in_proj(x, g, w_cat, w_alow, col_gelu, col_none, col_silu, tm=1024, tn=512):
    T, D = x.shape
    N = w_cat.shape[1]
    body = functools.partial(_in_proj_body, j_gelu=col_gelu // tn, j_none=col_none // tn,
                             j_silu=col_silu // tn)
    return pl.pallas_call(
        body,
        grid=(T // tm, N // tn),
        in_specs=[
            pl.BlockSpec((tm, D), lambda i, j: (i, 0)),
            pl.BlockSpec((1, D), lambda i, j: (0, 0)),
            pl.BlockSpec((D, tn), lambda i, j: (0, j)),
            pl.BlockSpec((D, LANES), lambda i, j: (0, 0)),
        ],
        out_specs=[
            pl.BlockSpec((tm, tn), lambda i, j: (i, j)),
            pl.BlockSpec((tm, LANES), lambda i, j: (i, 0)),
        ],
        out_shape=[jax.ShapeDtypeStruct((T, N), BF16), jax.ShapeDtypeStruct((T, LANES), F32)],
        scratch_shapes=[pltpu.VMEM((tm, D), BF16)],
        compiler_params=_params(("arbitrary", "arbitrary")),
        name="in_proj",
    )(x, g, w_cat, w_alow)


def _sgu_body(u_ref, v_ref, ga_ref, ws_ref, bs_ref, lng_ref, lnb_ref, wa_ref, o_ref, vn_scr, gated_scr):
    tm = u_ref.shape[0]
    gd = u_ref.shape[1] // SGU_GROUPS
    v = v_ref[...].astype(F32)
    mu = jnp.mean(v, axis=-1, keepdims=True)
    vc = v - mu
    var = jnp.mean(vc * vc, axis=-1, keepdims=True)
    vn_scr[...] = (vc * lax.rsqrt(var + EPS) * lng_ref[...] + lnb_ref[...]).astype(BF16)
    row = lax.broadcasted_iota(jnp.int32, (SGU_CHUNK, SGU_CHUNK), 0)
    col = lax.broadcasted_iota(jnp.int32, (SGU_CHUNK, SGU_CHUNK), 1)
    causal = row >= col
    for g in range(SGU_GROUPS):
        w = jnp.where(causal, ws_ref[g], 0.0).astype(BF16)
        bias = bs_ref[g]
        cs = slice(g * gd, (g + 1) * gd)
        for c in range(tm // SGU_CHUNK):
            rs = slice(c * SGU_CHUNK, (c + 1) * SGU_CHUNK)
            s = jnp.dot(w, vn_scr[rs, cs], preferred_element_type=F32) + bias
            gated_scr[rs, cs] = (u_ref[rs, cs].astype(F32) * s).astype(BF16)
    ya = jnp.dot(gated_scr[...], wa_ref[...], preferred_element_type=F32)
    o_ref[...] = (ga_ref[...].astype(F32) * ya).astype(BF16)


def _sgu(p, w_s, b_s, ln_g, ln_b, w_a, blk_u, blk_v, blk_gate, tm=256):
    T = p.shape[0]
    D = w_a.shape[0]
    G, C, _ = w_s.shape
    return pl.pallas_call(
        _sgu_body,
        grid=(T // tm,),
        in_specs=[
            pl.BlockSpec((tm, D), lambda i: (i, blk_u)),
            pl.BlockSpec((tm, D), lambda i: (i, blk_v)),
            pl.BlockSpec((tm, D), lambda i: (i, blk_gate)),
            pl.BlockSpec((G, C, C), lambda i: (0, 0, 0)),
            pl.BlockSpec((G, C, 1), lambda i: (0, 0, 0)),
            pl.BlockSpec((1, D), lambda i: (0, 0)),
            pl.BlockSpec((1, D), lambda i: (0, 0)),
            pl.BlockSpec((D, D), lambda i: (0, 0)),
        ],
        out_specs=pl.BlockSpec((tm, D), lambda i: (i, 0)),
        out_shape=jax.ShapeDtypeStruct((T, D), BF16),
        scratch_shapes=[pltpu.VMEM((tm, D), BF16), pltpu.VMEM((tm, D), BF16)],
        compiler_params=_params(("arbitrary",)),
        name="sgu",
    )(p, p, p, w_s, b_s, ln_g, ln_b, w_a)


def _gla_body(q_ref, k_ref, v_ref, r_ref, alow_ref, wup_ref, ba_ref, ng_ref, o_ref, st_scr):
    tc, dk = q_ref.shape

    @pl.when(pl.program_id(1) == 0)
    def _():
        st_scr[...] = jnp.zeros_like(st_scr)

    z = jnp.dot(alow_ref[...], wup_ref[...], preferred_element_type=F32,
                precision=lax.Precision.HIGHEST) + ba_ref[...]
    log_a = -(jnp.maximum(-z, 0.0) + jnp.log1p(jnp.exp(-jnp.abs(z)))) * (1.0 / GLA_GATE_TEMP)

    row = lax.broadcasted_iota(jnp.int32, (GLA_CHUNK, GLA_CHUNK), 0)
    col = lax.broadcasted_iota(jnp.int32, (GLA_CHUNK, GLA_CHUNK), 1)
    causal = row >= col
    tril = causal.astype(F32)
    scale = dk ** -0.5
    for c in range(tc // GLA_CHUNK):
        rs = slice(c * GLA_CHUNK, (c + 1) * GLA_CHUNK)
        b = jnp.dot(tril, log_a[rs], preferred_element_type=F32, precision=lax.Precision.HIGHEST)
        b_last = b[GLA_CHUNK - 1:GLA_CHUNK]
        q = q_ref[rs, :].astype(F32)
        k = k_ref[rs, :].astype(F32)
        v = v_ref[rs, :]
        qd = (q * (scale * jnp.exp(b))).astype(BF16)
        ki = (k * jnp.exp(-b)).astype(BF16)
        ke = (k * jnp.exp(b_last - b)).astype(BF16)
        att = lax.dot_general(qd, ki, NT_DIMS, preferred_element_type=F32)
        att = jnp.where(causal, att, 0.0).astype(BF16)
        st = st_scr[...]
        o = jnp.dot(att, v, preferred_element_type=F32)
        o = o + lax.dot_general(qd, st.astype(BF16), NT_DIMS, preferred_element_type=F32)
        st_scr[...] = st * jnp.exp(b_last) + lax.dot_general(v, ke, TN_DIMS, preferred_element_type=F32)
        o_ref[rs, :] = (_rms(o, ng_ref[...]) * r_ref[rs, :].astype(F32)).astype(BF16)


def _gla(p, alow, wup, b_alpha, norm_g, col_q, col_k, col_v, col_r, tc=256):
    T = p.shape[0]
    H = GLA_HEADS
    dk = wup.shape[1] // H
    dv = norm_g.shape[1]
    bq, bk, bv, br = col_q // dk, col_k // dk, col_v // dv, col_r // dv
    return pl.pallas_call(
        _gla_body,
        grid=(H, T // tc),
        in_specs=[
            pl.BlockSpec((tc, dk), lambda h, t: (t, bq + h)),
            pl.BlockSpec((tc, dk), lambda h, t: (t, bk + h)),
            pl.BlockSpec((tc, dv), lambda h, t: (t, bv + h)),
            pl.BlockSpec((tc, dv), lambda h, t: (t, br + h)),
            pl.BlockSpec((tc, LANES), lambda h, t: (t, 0)),
            pl.BlockSpec((LANES, dk), lambda h, t: (0, h)),
            pl.BlockSpec((1, dk), lambda h, t: (0, h)),
            pl.BlockSpec((1, dv), lambda h, t: (0, 0)),
        ],
        out_specs=pl.BlockSpec((tc, dv), lambda h, t: (t, h)),
        out_shape=jax.ShapeDtypeStruct((T, H * dv), BF16),
        scratch_shapes=[pltpu.VMEM((dv, dk), F32)],
        compiler_params=_params(("arbitrary", "arbitrary")),
        name="gla",
    )(p, p, p, p, alow, wup, b_alpha, norm_g)


def _merge_body(ob_ref, ya_ref, gb_ref, x_ref, wb_ref, wo_ref, gf_ref, x1_ref, h2_ref):
    yb = jnp.dot(ob_ref[...], wb_ref[...], preferred_element_type=F32)
    merged = ya_ref[...].astype(F32) + gb_ref[...].astype(F32) * yb
    x1 = x_ref[...] + jnp.dot(merged.astype(BF16), wo_ref[...], preferred_element_type=F32)
    x1_ref[...] = x1
    h2_ref[...] = _rms(x1, gf_ref[...]).astype(BF16)


def _merge(ob, ya, p, x, w_b, w_o, g_ffn, blk_gate, tm=256):
    T, D = x.shape
    return pl.pallas_call(
        _merge_body,
        grid=(T // tm,),
        in_specs=[
            pl.BlockSpec((tm, D), lambda i: (i, 0)),
            pl.BlockSpec((tm, D), lambda i: (i, 0)),
            pl.BlockSpec((tm, D), lambda i: (i, blk_gate)),
            pl.BlockSpec((tm, D), lambda i: (i, 0)),
            pl.BlockSpec((D, D), lambda i: (0, 0)),
            pl.BlockSpec((D, D), lambda i: (0, 0)),
            pl.BlockSpec((1, D), lambda i: (0, 0)),
        ],
        out_specs=[pl.BlockSpec((tm, D), lambda i: (i, 0)), pl.BlockSpec((tm, D), lambda i: (i, 0))],
        out_shape=[jax.ShapeDtypeStruct((T, D), F32), jax.ShapeDtypeStruct((T, D), BF16)],
        compiler_params=_params(("arbitrary",)),
        name="merge",
    )(ob, ya, p, x, w_b, w_o, g_ffn)


def _top_values(s, n, store):
    for i in range(n):
        m = jnp.max(s, axis=0, keepdims=True)
        store(i, m)
        s = jnp.where(s == m, -jnp.inf, s)


def _peer_score_body(h2_ref, wq_ref, sk_ref, s_ref, st_ref, top_scr):
    tt = h2_ref.shape[0]
    nk = PEER_N_KEYS
    q = jnp.dot(h2_ref[...], wq_ref[...], preferred_element_type=F32).astype(BF16)
    for hp in range(2 * PEER_HEADS):
        s_ref[hp] = lax.dot_general(sk_ref[hp], q[:, hp * nk:(hp + 1) * nk], NT_DIMS,
                                    preferred_element_type=F32)

    def head(h, carry):
        for p in range(2):
            def store(i, m, p=p):
                top_scr[p, i:i + 1, :] = m
            _top_values(s_ref[2 * h + p], PEER_TOPK, store)
        t1 = top_scr[0]
        t2 = top_scr[1]
        pieces = [t1[0:1] + t2]
        rowi = lax.broadcasted_iota(jnp.int32, (8, tt), 0)
        for i in range(1, PEER_TOPK):
            pieces.append(jnp.where(rowi < PEER_TOPK // (i + 1), t1[i:i + 1] + t2[0:8], -jnp.inf))
        best = []
        _top_values(jnp.concatenate(pieces, axis=0), PEER_TOPK, lambda i, m: best.append(m))
        z = jnp.ones_like(best[0])
        for bv in best[1:]:
            z = z + jnp.exp(bv - best[0])
        st_ref[0, pl.ds(h, 1), :] = best[PEER_TOPK - 1]
        st_ref[1, pl.ds(h, 1), :] = t1[0:1]
        st_ref[2, pl.ds(h, 1), :] = t2[0:1]
        st_ref[3, pl.ds(h, 1), :] = 1.0 / z
        return carry

    lax.fori_loop(0, PEER_HEADS, head, 0)


def _peer_score(h2, w_q, sub_keys, tt=256):
    T, D = h2.shape
    HP, nk, half = sub_keys.shape
    return pl.pallas_call(
        _peer_score_body,
        grid=(T // tt,),
        in_specs=[
            pl.BlockSpec((tt, D), lambda i: (i, 0)),
            pl.BlockSpec((D, HP * half), lambda i: (0, 0)),
            pl.BlockSpec((HP, nk, half), lambda i: (0, 0, 0)),
        ],
        out_specs=[
            pl.BlockSpec((HP, nk, tt), lambda i: (0, 0, i)),
            pl.BlockSpec((4, PEER_HEADS, tt), lambda i: (0, 0, i)),
        ],
        out_shape=[jax.ShapeDtypeStruct((HP, nk, T), F32), jax.ShapeDtypeStruct((4, PEER_HEADS, T), F32)],
        scratch_shapes=[pltpu.VMEM((2, PEER_TOPK, tt), F32)],
        compiler_params=_params(("arbitrary",)),
        name="peer_score",
    )(h2, w_q, sub_keys)


def _peer_dense_body(h2_ref, s_ref, st_ref, u_ref, vt_ref, o_ref, e2_scr, rows_scr, act_scr, w_scr):
    e = pl.program_id(1)
    ec, tt = act_scr.shape
    nk = PEER_N_KEYS
    na = ec // nk

    @pl.when(e == 0)
    def _():
        o_ref[...] = jnp.zeros_like(o_ref)
        for h in range(PEER_HEADS):
            e2_scr[h] = jnp.exp(s_ref[2 * h + 1] - st_ref[2, h:h + 1, :]) * st_ref[3, h:h + 1, :]

    a0 = pl.multiple_of(e * na, na)
    for h in range(PEER_HEADS):
        grp = s_ref[2 * h, pl.ds(a0, na), :]
        egrp = jnp.exp(grp - st_ref[1, h:h + 1, :])
        for al in range(na):
            rows_scr[2 * h, al, 0:1, :] = grp[al:al + 1, :]
            rows_scr[2 * h + 1, al, 0:1, :] = egrp[al:al + 1, :]

    act_scr[...] = lax.dot_general(u_ref[...], h2_ref[...], NT_DIMS, preferred_element_type=F32)

    def a_loop(al, carry):
        rows = pl.ds(pl.multiple_of(al * nk, nk), nk)
        for lb in range(tt // LANES):
            sl = slice(lb * LANES, (lb + 1) * LANES)
            g = jnp.zeros((nk, LANES), F32)
            for h in range(PEER_HEADS):
                v = rows_scr[2 * h, al, 0:1, sl] + s_ref[2 * h + 1, :, sl]
                pr = rows_scr[2 * h + 1, al, 0:1, sl] * e2_scr[h, :, sl]
                g = g + jnp.where(v >= st_ref[0, h:h + 1, sl], pr, 0.0)
            w_scr[rows, sl] = (_gelu(act_scr[rows, sl]) * g).astype(BF16)
        return carry

    lax.fori_loop(0, na, a_loop, 0)
    o_ref[...] += jnp.dot(vt_ref[...], w_scr[...], preferred_element_type=F32)


def _peer_dense(h2, s, st, u, vt, tt=512, ec=1024):
    T, D = h2.shape
    E = u.shape[0]
    HP, nk, _ = s.shape
    return pl.pallas_call(
        _peer_dense_body,
        grid=(T // tt, E // ec),
        in_specs=[
            pl.BlockSpec((tt, D), lambda i, e: (i, 0)),
            pl.BlockSpec((HP, nk, tt), lambda i, e: (0, 0, i)),
            pl.BlockSpec((4, PEER_HEADS, tt), lambda i, e: (0, 0, i)),
            pl.BlockSpec((ec, D), lambda i, e: (e, 0)),
            pl.BlockSpec((D, ec), lambda i, e: (0, e)),
        ],
        out_specs=pl.BlockSpec((D, tt), lambda i, e: (0, i)),
        out_shape=jax.ShapeDtypeStruct((D, T), F32),
        scratch_shapes=[pltpu.VMEM((PEER_HEADS, nk, tt), F32), pltpu.VMEM((HP, ec // nk, 8, tt), F32),
                        pltpu.VMEM((ec, tt), F32), pltpu.VMEM((ec, tt), BF16)],
        compiler_params=_params(("arbitrary", "arbitrary")),
        name="peer_dense",
    )(h2, s, st, u, vt)


def _final_body(x1_ref, pt_ref, g_ref, y_ref):
    y_ref[...] = _rms(x1_ref[...] + pt_ref[...].T, g_ref[...])


def _residual_body(x1_ref, pt_ref, y_ref):
    y_ref[...] = x1_ref[...] + pt_ref[...].T


def _final(x1, peer_t, g, tm=256):
    T, D = x1.shape
    specs = [pl.BlockSpec((tm, D), lambda i: (i, 0)), pl.BlockSpec((D, tm), lambda i: (0, i))]
    if g is None:
        body, args = _residual_body, (x1, peer_t)
    else:
        body, args = _final_body, (x1, peer_t, g)
        specs.append(pl.BlockSpec((1, D), lambda i: (0, 0)))
    return pl.pallas_call(
        body,
        grid=(T // tm,),
        in_specs=specs,
        out_specs=pl.BlockSpec((tm, D), lambda i: (i, 0)),
        out_shape=jax.ShapeDtypeStruct((T, D), F32),
        compiler_params=_params(("arbitrary",)),
        name="final",
    )(*args)


def kernel(x, g_mix, w_in, w_s, b_s, ln_v_g, ln_v_b, w_alpha_up, b_alpha, gla_norm_g, w_branch_a,
           w_branch_b, w_out, g_ffn, w_query, sub_keys, expert_u, expert_v, g_final):
    B, S, D = x.shape
    T = B * S
    depth = w_in.shape[0]
    sgu_w = w_branch_a.shape[1]
    key_w = w_alpha_up.shape[2]
    val_w = w_branch_b.shape[1]
    c_u, c_v = 0, sgu_w
    c_q = 2 * sgu_w
    c_k = c_q + key_w
    c_bv = c_k + key_w
    c_r = c_bv + val_w
    c_al = c_r + val_w
    c_ga = c_al + GLA_GATE_RANK
    c_ga_cat = c_al
    c_gb_cat = c_ga_cat + D

    xf = x.reshape(T, D)
    for l in range(depth):
        w_cat = jnp.concatenate([w_in[l][:, :c_al], w_in[l][:, c_ga:]], axis=1).astype(BF16)
        w_alow = jnp.pad(w_in[l][:, c_al:c_ga], ((0, 0), (0, LANES - GLA_GATE_RANK))).astype(BF16)
        p, alow = _in_proj(xf, g_mix[l][None], w_cat, w_alow, col_gelu=c_q, col_none=c_r, col_silu=c_al)

        ya = _sgu(p, w_s[l], b_s[l][..., None], ln_v_g[l][None], ln_v_b[l][None],
                  w_branch_a[l].astype(BF16), blk_u=c_u // D, blk_v=c_v // D, blk_gate=c_ga_cat // D)

        wup = jnp.pad(w_alpha_up[l], ((0, LANES - GLA_GATE_RANK), (0, 0)))
        ob = _gla(p, alow, wup, b_alpha[l][None], gla_norm_g[l][None], c_q, c_k, c_bv, c_r)

        x1, h2 = _merge(ob, ya, p, xf, w_branch_b[l].astype(BF16), w_out[l].astype(BF16), g_ffn[l][None],
                        blk_gate=c_gb_cat // D)

        hp = sub_keys.shape[1] * sub_keys.shape[2]
        sk = sub_keys[l].reshape(hp, sub_keys.shape[3], sub_keys.shape[4]).astype(BF16)
        s, st = _peer_score(h2, w_query[l].astype(BF16), sk)
        peer_t = _peer_dense(h2, s, st, expert_u[l].astype(BF16), expert_v[l].T.astype(BF16))
        xf = _final(x1, peer_t, g_final[None] if l == depth - 1 else None)
    return xf.reshape(B, S, D)
```

```python
import functools

import jax
import jax.numpy as jnp
from jax import lax
from jax.experimental import pallas as pl
from jax.experimental.pallas import tpu as pltpu

F32 = jnp.float32
BF16 = jnp.bfloat16

EPS = 1e-6
SGU_GROUPS = 8
SGU_CHUNK = 128
GLA_HEADS = 4
GLA_GATE_RANK = 16
GLA_GATE_TEMP = 16.0
GLA_CHUNK = 64
PEER_HEADS = 8
PEER_N_KEYS = 128
PEER_TOPK = 16
LANES = 128
GATE_ROWS = 32
PIPE_SLICES = 8
VMEM_LIMIT = 56 * 1024 * 1024

NT_DIMS = (((1,), (1,)), ((), ()))
TN_DIMS = (((0,), (0,)), ((), ()))


def _params(semantics, vmem=VMEM_LIMIT):
    return pltpu.CompilerParams(dimension_semantics=semantics, vmem_limit_bytes=vmem)


def _gelu(x):
    return 0.5 * x * (1.0 + lax.erf(x * (2.0 ** -0.5)))


def _rms(x, g):
    return x * lax.rsqrt(jnp.mean(x * x, axis=-1, keepdims=True) + EPS) * g


def _norm_body(x_ref, g_ref, h_ref):
    h_ref[...] = _rms(x_ref[...], g_ref[...]).astype(h_ref.dtype)


def _norm(x, g, tm=512):
    T, D = x.shape
    return pl.pallas_call(
        _norm_body,
        grid=(T // tm,),
        in_specs=[pl.BlockSpec((tm, D), lambda i: (i, 0)), pl.BlockSpec((1, D), lambda i: (0, 0))],
        out_specs=pl.BlockSpec((tm, D), lambda i: (i, 0)),
        out_shape=jax.ShapeDtypeStruct((T, D), BF16),
        compiler_params=_params(("arbitrary",)),
        name="norm",
    )(x, g)


_EPILOGUES = {
    "none": lambda a: a,
    "gelu": _gelu,
    "silu": lambda a: a * jax.nn.sigmoid(a),
    "sigmoid": jax.nn.sigmoid,
}


def _proj_body(h_ref, w_ref, o_ref, *, epilogue):
    acc = jnp.dot(h_ref[...], w_ref[...], preferred_element_type=F32)
    o_ref[...] = _EPILOGUES[epilogue](acc).astype(o_ref.dtype)


def _proj(h, w, col0, ncols, epilogue, out_dtype=BF16, tm=1024, tn=512):
    T, D = h.shape
    tn = min(tn, ncols)
    j0 = col0 // tn
    return pl.pallas_call(
        functools.partial(_proj_body, epilogue=epilogue),
        grid=(T // tm, ncols // tn),
        in_specs=[
            pl.BlockSpec((tm, D), lambda i, j: (i, 0)),
            pl.BlockSpec((D, tn), lambda i, j: (0, j0 + j)),
        ],
        out_specs=pl.BlockSpec((tm, tn), lambda i, j: (i, j)),
        out_shape=jax.ShapeDtypeStruct((T, ncols), out_dtype),
        compiler_params=_params(("arbitrary", "arbitrary")),
        name="proj_" + epilogue,
    )(h, w)


def _sgu_body(u_ref, v_ref, ga_ref, ws_ref, bs_ref, lng_ref, lnb_ref, wa_ref, o_ref, vn_scr, gated_scr):
    tm = u_ref.shape[0]
    gd = u_ref.shape[1] // SGU_GROUPS
    v = v_ref[...].astype(F32)
    mu = jnp.mean(v, axis=-1, keepdims=True)
    vc = v - mu
    var = jnp.mean(vc * vc, axis=-1, keepdims=True)
    vn_scr[...] = (vc * lax.rsqrt(var + EPS) * lng_ref[...] + lnb_ref[...]).astype(BF16)
    row = lax.broadcasted_iota(jnp.int32, (SGU_CHUNK, SGU_CHUNK), 0)
    col = lax.broadcasted_iota(jnp.int32, (SGU_CHUNK, SGU_CHUNK), 1)
    causal = row >= col
    for g in range(SGU_GROUPS):
        w = jnp.where(causal, ws_ref[g], 0.0).astype(BF16)
        bias = bs_ref[g]
        cs = slice(g * gd, (g + 1) * gd)
        for c in range(tm // SGU_CHUNK):
            rs = slice(c * SGU_CHUNK, (c + 1) * SGU_CHUNK)
            s = jnp.dot(w, vn_scr[rs, cs], preferred_element_type=F32) + bias
            gated_scr[rs, cs] = (u_ref[rs, cs].astype(F32) * s).astype(BF16)
    ya = jnp.dot(gated_scr[...], wa_ref[...], preferred_element_type=F32)
    o_ref[...] = (ga_ref[...].astype(F32) * ya).astype(BF16)


def _sgu(p_uv, p_gates, w_s, b_s, ln_g, ln_b, w_a, tm=256):
    T = p_uv.shape[0]
    D = w_a.shape[0]
    G, C, _ = w_s.shape
    return pl.pallas_call(
        _sgu_body,
        grid=(T // tm,),
        in_specs=[
            pl.BlockSpec((tm, D), lambda i: (i, 0)),
            pl.BlockSpec((tm, D), lambda i: (i, 1)),
            pl.BlockSpec((tm, D), lambda i: (i, 0)),
            pl.BlockSpec((G, C, C), lambda i: (0, 0, 0)),
            pl.BlockSpec((G, C, 1), lambda i: (0, 0, 0)),
            pl.BlockSpec((1, D), lambda i: (0, 0)),
            pl.BlockSpec((1, D), lambda i: (0, 0)),
            pl.BlockSpec((D, D), lambda i: (0, 0)),
        ],
        out_specs=pl.BlockSpec((tm, D), lambda i: (i, 0)),
        out_shape=jax.ShapeDtypeStruct((T, D), BF16),
        scratch_shapes=[pltpu.VMEM((tm, D), BF16), pltpu.VMEM((tm, D), BF16)],
        compiler_params=_params(("arbitrary",)),
        name="sgu",
    )(p_uv, p_uv, p_gates, w_s, b_s, ln_g, ln_b, w_a)


def _gla_body(q_ref, k_ref, v_ref, r_ref, alow_ref, wup_ref, ba_ref, ng_ref, o_ref, st_scr):
    tc, dk = q_ref.shape

    @pl.when(pl.program_id(1) == 0)
    def _():
        st_scr[...] = jnp.zeros_like(st_scr)

    z = jnp.dot(alow_ref[...], wup_ref[...], preferred_element_type=F32,
                precision=lax.Precision.HIGHEST) + ba_ref[...]
    log_a = -(jnp.maximum(-z, 0.0) + jnp.log1p(jnp.exp(-jnp.abs(z)))) * (1.0 / GLA_GATE_TEMP)

    row = lax.broadcasted_iota(jnp.int32, (GLA_CHUNK, GLA_CHUNK), 0)
    col = lax.broadcasted_iota(jnp.int32, (GLA_CHUNK, GLA_CHUNK), 1)
    causal = row >= col
    tril = causal.astype(F32)
    scale = dk ** -0.5
    for c in range(tc // GLA_CHUNK):
        rs = slice(c * GLA_CHUNK, (c + 1) * GLA_CHUNK)
        b = jnp.dot(tril, log_a[rs], preferred_element_type=F32, precision=lax.Precision.HIGHEST)
        b_last = b[GLA_CHUNK - 1:GLA_CHUNK]
        q = q_ref[rs, :].astype(F32)
        k = k_ref[rs, :].astype(F32)
        v = v_ref[rs, :]
        qd = (q * (scale * jnp.exp(b))).astype(BF16)
        ki = (k * jnp.exp(-b)).astype(BF16)
        ke = (k * jnp.exp(b_last - b)).astype(BF16)
        att = lax.dot_general(qd, ki, NT_DIMS, preferred_element_type=F32)
        att = jnp.where(causal, att, 0.0).astype(BF16)
        st = st_scr[...]
        o = jnp.dot(att, v, preferred_element_type=F32)
        o = o + lax.dot_general(qd, st.astype(BF16), NT_DIMS, preferred_element_type=F32)
        st_scr[...] = st * jnp.exp(b_last) + lax.dot_general(v, ke, TN_DIMS, preferred_element_type=F32)
        o_ref[rs, :] = (_rms(o, ng_ref[...]) * r_ref[rs, :].astype(F32)).astype(BF16)


def _gla(p_qkv, p_r, alow, wup, b_alpha, norm_g, tc=256):
    T = p_qkv.shape[0]
    H = GLA_HEADS
    dk = wup.shape[1] // H
    dv = norm_g.shape[1]
    bk, bv = H, 2 * H * dk // dv
    return pl.pallas_call(
        _gla_body,
        grid=(H, T // tc),
        in_specs=[
            pl.BlockSpec((tc, dk), lambda h, t: (t, h)),
            pl.BlockSpec((tc, dk), lambda h, t: (t, bk + h)),
            pl.BlockSpec((tc, dv), lambda h, t: (t, bv + h)),
            pl.BlockSpec((tc, dv), lambda h, t: (t, h)),
            pl.BlockSpec((tc, LANES), lambda h, t: (t, 0)),
            pl.BlockSpec((LANES, dk), lambda h, t: (0, h)),
            pl.BlockSpec((1, dk), lambda h, t: (0, h)),
            pl.BlockSpec((1, dv), lambda h, t: (0, 0)),
        ],
        out_specs=pl.BlockSpec((tc, dv), lambda h, t: (t, h)),
        out_shape=jax.ShapeDtypeStruct((T, H * dv), BF16),
        scratch_shapes=[pltpu.VMEM((dv, dk), F32)],
        compiler_params=_params(("arbitrary", "arbitrary")),
        name="gla",
    )(p_qkv, p_qkv, p_qkv, p_r, alow, wup, b_alpha, norm_g)


def _merge_body(ob_ref, ya_ref, gb_ref, x_ref, wb_ref, wo_ref, gf_ref, x1_ref, h2_ref):
    yb = jnp.dot(ob_ref[...], wb_ref[...], preferred_element_type=F32)
    merged = ya_ref[...].astype(F32) + gb_ref[...].astype(F32) * yb
    x1 = x_ref[...] + jnp.dot(merged.astype(BF16), wo_ref[...], preferred_element_type=F32)
    x1_ref[...] = x1
    h2_ref[...] = _rms(x1, gf_ref[...]).astype(BF16)


def _merge(ob, ya, p_gates, x, w_b, w_o, g_ffn, tm=256):
    T, D = x.shape
    return pl.pallas_call(
        _merge_body,
        grid=(T // tm,),
        in_specs=[
            pl.BlockSpec((tm, D), lambda i: (i, 0)),
            pl.BlockSpec((tm, D), lambda i: (i, 0)),
            pl.BlockSpec((tm, D), lambda i: (i, 1)),
            pl.BlockSpec((tm, D), lambda i: (i, 0)),
            pl.BlockSpec((D, D), lambda i: (0, 0)),
            pl.BlockSpec((D, D), lambda i: (0, 0)),
            pl.BlockSpec((1, D), lambda i: (0, 0)),
        ],
        out_specs=[pl.BlockSpec((tm, D), lambda i: (i, 0)), pl.BlockSpec((tm, D), lambda i: (i, 0))],
        out_shape=[jax.ShapeDtypeStruct((T, D), F32), jax.ShapeDtypeStruct((T, D), BF16)],
        compiler_params=_params(("arbitrary",)),
        name="merge",
    )(ob, ya, p_gates, x, w_b, w_o, g_ffn)


def _top_values(s, n, store):
    for i in range(n):
        m = jnp.max(s, axis=0, keepdims=True)
        store(i, m)
        s = jnp.where(s == m, -jnp.inf, s)


def _peer_score_body(h2_ref, wq_ref, sk_ref, s_ref, st_ref, top_scr):
    tt = h2_ref.shape[0]
    nk = PEER_N_KEYS
    q = jnp.dot(h2_ref[...], wq_ref[...], preferred_element_type=F32).astype(BF16)
    for hp in range(2 * PEER_HEADS):
        s_ref[hp] = lax.dot_general(sk_ref[hp], q[:, hp * nk:(hp + 1) * nk], NT_DIMS,
                                    preferred_element_type=F32)

    def head(h, carry):
        for p in range(2):
            def store(i, m, p=p):
                top_scr[p, i:i + 1, :] = m
            _top_values(s_ref[2 * h + p], PEER_TOPK, store)
        t1 = top_scr[0]
        t2 = top_scr[1]
        pieces = [t1[0:1] + t2]
        rowi = lax.broadcasted_iota(jnp.int32, (8, tt), 0)
        for i in range(1, PEER_TOPK):
            pieces.append(jnp.where(rowi < PEER_TOPK // (i + 1), t1[i:i + 1] + t2[0:8], -jnp.inf))
        best = []
        _top_values(jnp.concatenate(pieces, axis=0), PEER_TOPK, lambda i, m: best.append(m))
        z = jnp.ones_like(best[0])
        for bv in best[1:]:
            z = z + jnp.exp(bv - best[0])
        st_ref[0, pl.ds(h, 1), :] = best[PEER_TOPK - 1]
        st_ref[1, pl.ds(h, 1), :] = t1[0:1]
        st_ref[2, pl.ds(h, 1), :] = t2[0:1]
        st_ref[3, pl.ds(h, 1), :] = 1.0 / z
        return carry

    lax.fori_loop(0, PEER_HEADS, head, 0)


def _peer_score(h2, w_q, sub_keys, tt=256):
    T, D = h2.shape
    HP, nk, half = sub_keys.shape
    return pl.pallas_call(
        _peer_score_body,
        grid=(T // tt,),
        in_specs=[
            pl.BlockSpec((tt, D), lambda i: (i, 0)),
            pl.BlockSpec((D, HP * half), lambda i: (0, 0)),
            pl.BlockSpec((HP, nk, half), lambda i: (0, 0, 0)),
        ],
        out_specs=[
            pl.BlockSpec((HP, nk, tt), lambda i: (0, 0, i)),
            pl.BlockSpec((4, PEER_HEADS, tt), lambda i: (0, 0, i)),
        ],
        out_shape=[jax.ShapeDtypeStruct((HP, nk, T), F32), jax.ShapeDtypeStruct((4, PEER_HEADS, T), F32)],
        scratch_shapes=[pltpu.VMEM((2, PEER_TOPK, tt), F32)],
        compiler_params=_params(("arbitrary",)),
        name="peer_score",
    )(h2, w_q, sub_keys)


def _peer_dense_body(h2_ref, s1_ref, s2_ref, st_ref, u_ref, v_ref, x1_ref, gf_ref, o_ref, e2_scr, rows_scr,
                     act_scr, w_scr, *, final_norm):
    e = pl.program_id(1)
    ec, tt = act_scr.shape
    nk = PEER_N_KEYS
    na = ec // nk

    @pl.when(e == 0)
    def _():
        o_ref[...] = jnp.zeros_like(o_ref)
        for h in range(PEER_HEADS):
            e2_scr[h] = jnp.exp(s2_ref[h] - st_ref[2, h:h + 1, :]) * st_ref[3, h:h + 1, :]

    for h in range(PEER_HEADS):
        grp = s1_ref[h]
        egrp = jnp.exp(grp - st_ref[1, h:h + 1, :])
        for al in range(na):
            rows_scr[2 * h, al, 0:1, :] = grp[al:al + 1, :]
            rows_scr[2 * h + 1, al, 0:1, :] = egrp[al:al + 1, :]

    act_scr[...] = lax.dot_general(u_ref[...], h2_ref[...], NT_DIMS, preferred_element_type=F32)

    def a_loop(al, carry):
        for lb in range(tt // LANES):
            sl = slice(lb * LANES, (lb + 1) * LANES)
            for rb in range(nk // GATE_ROWS):
                bs = slice(rb * GATE_ROWS, (rb + 1) * GATE_ROWS)
                rows = pl.ds(pl.multiple_of(al * nk + rb * GATE_ROWS, GATE_ROWS), GATE_ROWS)
                g = jnp.zeros((GATE_ROWS, LANES), F32)
                for h in range(PEER_HEADS):
                    v = rows_scr[2 * h, al, 0:1, sl] + s2_ref[h, bs, sl]
                    pr = rows_scr[2 * h + 1, al, 0:1, sl] * e2_scr[h, bs, sl]
                    g = g + jnp.where(v >= st_ref[0, h:h + 1, sl], pr, 0.0)
                w_scr[rows, sl] = (_gelu(act_scr[rows, sl]) * g).astype(BF16)
        return carry

    lax.fori_loop(0, na, a_loop, 0)
    o_ref[...] += lax.dot_general(w_scr[...], v_ref[...], TN_DIMS, preferred_element_type=F32)

    @pl.when(e == pl.num_programs(1) - 1)
    def _():
        y = x1_ref[...] + o_ref[...]
        o_ref[...] = _rms(y, gf_ref[...]) if final_norm else y


def _peer_dense(h2, s, st, u, v, x1, g_final, final_norm, tt=512, ec=1024):
    T, D = h2.shape
    E = u.shape[0]
    HP, nk, _ = s.shape
    na = ec // nk
    s4 = s.reshape(PEER_HEADS, 2, nk, T)
    return pl.pallas_call(
        functools.partial(_peer_dense_body, final_norm=final_norm),
        grid=(T // tt, E // ec),
        in_specs=[
            pl.BlockSpec((tt, D), lambda i, e: (i, 0)),
            pl.BlockSpec((PEER_HEADS, None, na, tt), lambda i, e: (0, 0, e, i)),
            pl.BlockSpec((PEER_HEADS, None, nk, tt), lambda i, e: (0, 1, 0, i)),
            pl.BlockSpec((4, PEER_HEADS, tt), lambda i, e: (0, 0, i)),
            pl.BlockSpec((ec, D), lambda i, e: (e, 0)),
            pl.BlockSpec((ec, D), lambda i, e: (e, 0)),
            pl.BlockSpec((tt, D), lambda i, e: (i, 0)),
            pl.BlockSpec((1, D), lambda i, e: (0, 0)),
        ],
        out_specs=pl.BlockSpec((tt, D), lambda i, e: (i, 0)),
        out_shape=jax.ShapeDtypeStruct((T, D), F32),
        scratch_shapes=[pltpu.VMEM((PEER_HEADS, nk, tt), F32), pltpu.VMEM((HP, ec // nk, 8, tt), F32),
                        pltpu.VMEM((ec, tt), F32), pltpu.VMEM((ec, tt), BF16)],
        compiler_params=_params(("arbitrary", "arbitrary")),
        name="peer_dense",
    )(h2, s4, s4, st, u, v, x1, g_final)


def kernel(x, g_mix, w_in, w_s, b_s, ln_v_g, ln_v_b, w_alpha_up, b_alpha, gla_norm_g, w_branch_a,
           w_branch_b, w_out, g_ffn, w_query, sub_keys, expert_u, expert_v, g_final):
    B, S, D = x.shape
    T = B * S
    depth = w_in.shape[0]
    sgu_w = w_branch_a.shape[1]
    key_w = w_alpha_up.shape[2]
    val_w = w_branch_b.shape[1]
    c_q = 2 * sgu_w
    c_r = c_q + 2 * key_w + val_w
    c_al = c_r + val_w
    c_ga = c_al + GLA_GATE_RANK

    xf = x.reshape(T, D)
    for l in range(depth):
        w_main = w_in[l].astype(BF16)
        w_gates = w_in[l][:, c_ga:].astype(BF16)
        w_alow = jnp.pad(w_in[l][:, c_al:c_ga], ((0, 0), (0, LANES - GLA_GATE_RANK))).astype(BF16)
        h = _norm(xf, g_mix[l][None])
        p_uv = _proj(h, w_main, 0, c_q, "gelu")
        p_qkv = _proj(h, w_main, c_q, c_r - c_q, "none")
        p_r = _proj(h, w_main, c_r, c_al - c_r, "silu")
        p_gates = _proj(h, w_gates, 0, 2 * D, "sigmoid")
        alow = _proj(h, w_alow, 0, LANES, "none", out_dtype=F32)

        ya = _sgu(p_uv, p_gates, w_s[l], b_s[l][..., None], ln_v_g[l][None], ln_v_b[l][None],
                  w_branch_a[l].astype(BF16))

        wup = jnp.pad(w_alpha_up[l], ((0, LANES - GLA_GATE_RANK), (0, 0)))
        ob = _gla(p_qkv, p_r, alow, wup, b_alpha[l][None], gla_norm_g[l][None])

        x1, h2 = _merge(ob, ya, p_gates, xf, w_branch_b[l].astype(BF16), w_out[l].astype(BF16), g_ffn[l][None])

        hp = sub_keys.shape[1] * sub_keys.shape[2]
        sk = sub_keys[l].reshape(hp, sub_keys.shape[3], sub_keys.shape[4]).astype(BF16)
        s, st = _peer_score(h2, w_query[l].astype(BF16), sk)
        xf = _peer_dense(h2, s, st, expert_u[l].astype(BF16), expert_v[l].astype(BF16), x1, g_final[None],
                         final_norm=(l == depth - 1))
    return xf.reshape(B, S, D)
```
